```python
import jax, jax.numpy as jnp
from jax import lax
import numpy as np

D_MODEL = 1024
BATCH = 2
SEQ = 8192
DEPTH = 4

N_META = 16
CHUNK = 64
EPS = 1e-6

HG_HEADS = 4
HG_DIM = 128
HG_WIDTH = HG_HEADS * HG_DIM
RET_HEADS = 4
RET_DK = 32
RET_DV = 64
RET_QK_WIDTH = RET_HEADS * RET_DK
RET_V_WIDTH = RET_HEADS * RET_DV
ROPE_BASE = 10000.0
LRU_WIDTH = 256
LRU_BLOCKS = 4
LRU_BLOCK = LRU_WIDTH // LRU_BLOCKS
LRU_CONV = 4
LRU_C = 8.0
D_MIX = HG_WIDTH + RET_V_WIDTH + LRU_WIDTH
IN_SIZES = (HG_WIDTH, HG_WIDTH, HG_WIDTH, HG_WIDTH,
            RET_QK_WIDTH, RET_QK_WIDTH, RET_V_WIDTH, RET_V_WIDTH,
            LRU_WIDTH, LRU_WIDTH)
IN_COLS = sum(IN_SIZES)
IN_SPLITS = tuple(int(s) for s in np.cumsum(IN_SIZES)[:-1])
D_FF = 2816
FFN_CONV = 3

kernel_name = "hymba_hgrn2_retention_rglru_convffn"


def rmsnorm(x, g):
    xf = x.astype(jnp.float32)
    y = xf * lax.rsqrt(jnp.mean(xf * xf, axis=-1, keepdims=True) + EPS)
    return (y * g.astype(jnp.float32)).astype(x.dtype)


def head_groupnorm(o, g):
    of = o.astype(jnp.float32)
    mu = jnp.mean(of, axis=-1, keepdims=True)
    var = jnp.mean(jnp.square(of - mu), axis=-1, keepdims=True)
    return ((of - mu) * lax.rsqrt(var + EPS) * g.astype(jnp.float32)).astype(o.dtype)


def causal_dwconv(x, w, b):
    width, ch = w.shape
    y = lax.conv_general_dilated(
        x, w[:, None, :].astype(x.dtype), window_strides=(1,), padding=[(width - 1, 0)],
        dimension_numbers=('NWC', 'WIO', 'NWC'), feature_group_count=ch)
    return y + b.astype(x.dtype)


def split_heads(a, n):
    bsz, t, _ = a.shape
    return a.reshape(bsz, t, n, -1).transpose(0, 2, 1, 3)


def merge_heads(a):
    bsz, h, t, d = a.shape
    return a.transpose(0, 2, 1, 3).reshape(bsz, t, h * d)


def to_chunks(a, pad):
    bsz, h, _, d = a.shape
    a = jnp.pad(a, ((0, 0), (0, 0), (pad, 0), (0, 0)))
    return a.reshape(bsz, h, -1, CHUNK, d).transpose(2, 0, 1, 3, 4)


def from_chunks(o, pad):
    n, bsz, h, c, d = o.shape
    return o.transpose(1, 2, 0, 3, 4).reshape(bsz, h, n * c, d)[:, :, pad:]


def hgrn2_chunked(q, k, v, log_f):
    bsz, h, t, dk = q.shape
    dv = v.shape[-1]
    pad = (-t) % CHUNK
    qc, kc, vc, gc = (to_chunks(a, pad) for a in (q, k, v, log_f))
    causal = jnp.tril(jnp.ones((CHUNK, CHUNK), dtype=bool))[:, :, None]

    def step(state, inp):
        qi, ki, vi, gi = inp
        b = jnp.cumsum(gi.astype(jnp.float32), axis=-2)
        diff = b[:, :, :, None, :] - b[:, :, None, :, :]
        decay = jnp.exp(jnp.where(causal, diff, -jnp.inf))
        scores = jnp.einsum('bhtd,bhtsd,bhsd->bhts', qi, decay, ki)
        o = jnp.einsum('bhts,bhse->bhte', scores, vi) + \
            jnp.einsum('bhtd,bhde->bhte', qi * jnp.exp(b), state)
        b_last = b[:, :, -1:, :]
        state = jnp.exp(b_last[:, :, 0, :])[..., None] * state + \
            jnp.einsum('bhsd,bhse->bhde', ki * jnp.exp(b_last - b), vi)
        return state, o

    s0 = jnp.zeros((bsz, h, dk, dv), jnp.float32)
    _, o = lax.scan(step, s0, (qc, kc, vc, gc))
    return from_chunks(o, pad)


def retention_chunked(q, k, v, log_gamma):
    bsz, h, t, dk = q.shape
    dv = v.shape[-1]
    pad = (-t) % CHUNK
    qc, kc, vc = (to_chunks(a, pad).transpose(1, 2, 0, 3, 4) for a in (q, k, v))
    idx = jnp.arange(CHUNK, dtype=jnp.float32)
    lg = log_gamma[:, None]
    rel = idx[:, None] - idx[None, :]
    dmat = jnp.where(rel >= 0, jnp.exp(lg[:, :, None] * jnp.maximum(rel, 0.0)), 0.0)
    scores = jnp.einsum('bhntd,bhnsd->bhnts', qc, kc) * dmat[None, :, None]
    intra = jnp.einsum('bhnts,bhnse->bhnte', scores, vc)
    w_state = jnp.exp(lg * (CHUNK - 1 - idx))
    kv = jnp.einsum('bhnsd,bhnse->nbhde', kc * w_state[None, :, None, :, None], vc)
    chunk_decay = jnp.exp(log_gamma * CHUNK)[None, :, None, None]

    def step(state, kv_n):
        return chunk_decay * state + kv_n, state

    _, r_prev = lax.scan(step, jnp.zeros((bsz, h, dk, dv), jnp.float32), kv)
    w_q = jnp.exp(lg * (idx + 1.0))
    inter = jnp.einsum('bhntd,nbhde->bhnte', qc * w_q[None, :, None, :, None], r_prev)
    o = (intra + inter).transpose(2, 0, 1, 3, 4)
    return from_chunks(o, pad)


def rope(x, cos, sin):
    x1, x2 = jnp.split(x, 2, axis=-1)
    return jnp.concatenate([x1 * cos - x2 * sin, x1 * sin + x2 * cos], axis=-1)


def rg_lru(x, gate_a_w, gate_a_b, gate_i_w, gate_i_b, lam):
    bsz, t, w = x.shape
    xb = x.reshape(bsz, t, LRU_BLOCKS, LRU_BLOCK)
    r = jax.nn.sigmoid(jnp.einsum('btnd,nde->btne', xb, gate_a_w).reshape(bsz, t, w) + gate_a_b)
    i = jax.nn.sigmoid(jnp.einsum('btnd,nde->btne', xb, gate_i_w).reshape(bsz, t, w) + gate_i_b)
    log_a = -LRU_C * r.astype(jnp.float32) * jax.nn.softplus(-lam.astype(jnp.float32))
    a = jnp.exp(log_a)
    u = jnp.sqrt(-jnp.expm1(2.0 * log_a)) * (i * x)

    def combine(c1, c2):
        a1, b1 = c1
        a2, b2 = c2
        return a1 * a2, a2 * b1 + b2

    _, hseq = lax.associative_scan(combine, (a, u), axis=1)
    return hseq.astype(x.dtype)


def hybrid_mixer(h, w_in, lb, hg_norm_g, ret_norm_g, lru_conv_w, lru_conv_b,
                 gate_a_w, gate_a_b, gate_i_w, gate_i_b, lru_lambda, w_out, cos, sin):
    proj = jnp.einsum('btd,dc->btc', h, w_in)
    hq, hf, hi, hg, rq, rk, rv, rg, lx, ly = jnp.split(proj, IN_SPLITS, axis=-1)

    z = hf.astype(jnp.float32)
    log_f = jnp.logaddexp(jnp.log(lb), jnp.log1p(-lb) + jax.nn.log_sigmoid(z))
    k_hg = (1.0 - lb) * jax.nn.sigmoid(-z)
    q_hg = jax.nn.silu(hq) * HG_DIM ** -0.5
    o_hg = hgrn2_chunked(split_heads(q_hg, HG_HEADS), split_heads(k_hg, HG_HEADS),
                         split_heads(hi, HG_HEADS), split_heads(log_f, HG_HEADS))
    o_hg = merge_heads(rmsnorm(o_hg, hg_norm_g)) * jax.nn.silu(hg)

    log_gamma = jnp.log1p(-jnp.exp2(-5.0 - jnp.arange(RET_HEADS, dtype=jnp.float32)))
    q_r = rope(split_heads(rq, RET_HEADS), cos, sin)
    k_r = rope(split_heads(rk, RET_HEADS), cos, sin) * RET_DK ** -0.5
    o_ret = retention_chunked(q_r, k_r, split_heads(rv, RET_HEADS), log_gamma)
    o_ret = merge_heads(head_groupnorm(o_ret, ret_norm_g)) * jax.nn.silu(rg)

    xc = causal_dwconv(lx, lru_conv_w, lru_conv_b)
    o_lru = rg_lru(xc, gate_a_w, gate_a_b, gate_i_w, gate_i_b, lru_lambda) * jax.nn.gelu(ly)

    merged = jnp.concatenate([o_hg, o_ret, o_lru], axis=-1)
    return jnp.einsum('btc,cd->btd', merged, w_out)


def conv_glu_ffn(h, w_up, conv_w, conv_b, w_down):
    u = causal_dwconv(jnp.einsum('btd,df->btf', h, w_up), conv_w, conv_b)
    gate, val = jnp.split(u, 2, axis=-1)
    return jnp.einsum('btf,fd->btd', jax.nn.silu(gate) * val, w_down)


def setup_inputs(seed: int = 0) -> dict:
    key = jax.random.key(seed)
    ks = jax.random.split(key, 24)
    f32 = jnp.float32

    def nrm(k, shape, scale):
        return scale * jax.random.normal(k, shape, f32)

    a_c = jax.random.uniform(ks[10], (DEPTH, LRU_WIDTH), f32, 0.9, 0.999)
    s = a_c ** (1.0 / LRU_C)
    return {
        "x": nrm(ks[0], (BATCH, SEQ, D_MODEL), 1.0),
        "meta_tokens": nrm(ks[1], (N_META, D_MODEL), 1.0),
        "norm_mix_g": 1.0 + nrm(ks[2], (DEPTH, D_MODEL), 0.02),
        "w_in": nrm(ks[3], (DEPTH, D_MODEL, IN_COLS), D_MODEL ** -0.5),
        "hg_lb_logits": nrm(ks[4], (DEPTH, HG_WIDTH), 0.1),
        "hg_norm_g": 1.0 + nrm(ks[5], (DEPTH, HG_DIM), 0.02),
        "ret_norm_g": 1.0 + nrm(ks[6], (DEPTH, RET_DV), 0.02),
        "lru_conv_w": nrm(ks[7], (DEPTH, LRU_CONV, LRU_WIDTH), LRU_CONV ** -0.5),
        "lru_conv_b": nrm(ks[8], (DEPTH, LRU_WIDTH), 0.01),
        "lru_gate_a_w": nrm(ks[9], (DEPTH, LRU_BLOCKS, LRU_BLOCK, LRU_BLOCK), LRU_BLOCK ** -0.5),
        "lru_gate_a_b": nrm(ks[11], (DEPTH, LRU_WIDTH), 0.01),
        "lru_gate_i_w": nrm(ks[12], (DEPTH, LRU_BLOCKS, LRU_BLOCK, LRU_BLOCK), LRU_BLOCK ** -0.5),
        "lru_gate_i_b": nrm(ks[13], (DEPTH, LRU_WIDTH), 0.01),
        "lru_lambda": jnp.log(s) - jnp.log1p(-s),
        "w_out": nrm(ks[14], (DEPTH, D_MIX, D_MODEL), D_MIX ** -0.5),
        "norm_ffn_g": 1.0 + nrm(ks[15], (DEPTH, D_MODEL), 0.02),
        "ffn_w_up": nrm(ks[16], (DEPTH, D_MODEL, 2 * D_FF), D_MODEL ** -0.5),
        "ffn_conv_w": nrm(ks[17], (DEPTH, FFN_CONV, 2 * D_FF), FFN_CONV ** -0.5),
        "ffn_conv_b": nrm(ks[18], (DEPTH, 2 * D_FF), 0.01),
        "ffn_w_down": nrm(ks[19], (DEPTH, D_FF, D_MODEL), D_FF ** -0.5),
        "final_norm_g": 1.0 + nrm(ks[20], (D_MODEL,), 0.02),
    }


def reference(x, meta_tokens, norm_mix_g, w_in, hg_lb_logits, hg_norm_g, ret_norm_g,
              lru_conv_w, lru_conv_b, lru_gate_a_w, lru_gate_a_b, lru_gate_i_w, lru_gate_i_b,
              lru_lambda, w_out, norm_ffn_g, ffn_w_up, ffn_conv_w, ffn_conv_b, ffn_w_down,
              final_norm_g):
    bsz = x.shape[0]
    meta = jnp.broadcast_to(meta_tokens[None].astype(x.dtype), (bsz, N_META, x.shape[-1]))
    h = jnp.concatenate([meta, x], axis=1)
    t = h.shape[1]

    pos = jnp.arange(t, dtype=jnp.float32)
    theta = ROPE_BASE ** (-jnp.linspace(0.0, 1.0, RET_DK // 2, dtype=jnp.float32))
    ang = pos[:, None] * theta[None, :]
    cos, sin = jnp.cos(ang), jnp.sin(ang)

    lb_all = jnp.cumsum(jax.nn.softmax(hg_lb_logits.astype(jnp.float32), axis=0), axis=0)
    lb_all = lb_all - lb_all[0:1]

    for l in range(DEPTH):
        h = h + hybrid_mixer(rmsnorm(h, norm_mix_g[l]), w_in[l], lb_all[l], hg_norm_g[l],
                             ret_norm_g[l], lru_conv_w[l], lru_conv_b[l], lru_gate_a_w[l],
                             lru_gate_a_b[l], lru_gate_i_w[l], lru_gate_i_b[l], lru_lambda[l],
                             w_out[l], cos, sin).astype(h.dtype)
        h = h + conv_glu_ffn(rmsnorm(h, norm_ffn_g[l]), ffn_w_up[l], ffn_conv_w[l],
                             ffn_conv_b[l], ffn_w_down[l]).astype(h.dtype)

    return rmsnorm(h, final_norm_g)[:, N_META:]
```

```python
import functools

import jax
import jax.numpy as jnp
from jax import lax
from jax.experimental import pallas as pl
from jax.experimental.pallas import tpu as pltpu

F32 = jnp.float32
BF16 = jnp.bfloat16

D_MODEL = 1024
N_META = 16
EPS = 1e-6
HG_HEADS = 4
HG_DIM = 128
HG_WIDTH = HG_HEADS * HG_DIM
RET_HEADS = 4
RET_DK = 32
RET_DV = 64
RET_QK = RET_HEADS * RET_DK
RET_V = RET_HEADS * RET_DV
ROPE_BASE = 10000.0
LRU_W = 256
LRU_BLOCKS = 4
LRU_CONV = 4
LRU_C = 8.0
D_FF = 2816
FFN_CONV = 3
HG_COLS = 4 * HG_WIDTH
RET_COLS = 2 * RET_QK + 2 * RET_V
LRU_COLS = 2 * LRU_W
IN_COLS = HG_COLS + RET_COLS + LRU_COLS

SUBLANES = 8
DIAG = SUBLANES
SEQ_BLOCK = 256
CHUNK = 128
FFN_COLS = 256
N_SLABS = D_FF // FFN_COLS
CONV_PAD = SUBLANES
VMEM_LIMIT = 56 * 1024 * 1024


def _dot(a, b):
    return jnp.dot(a, b, preferred_element_type=F32)


def _dot_nt(a, b):
    return lax.dot_general(a, b, (((1,), (1,)), ((), ())), preferred_element_type=F32)


def _dot_tn(a, b):
    return lax.dot_general(a, b, (((0,), (0,)), ((), ())), preferred_element_type=F32)


def _sigmoid(x):
    return 0.5 * jnp.tanh(0.5 * x) + 0.5


def _silu(x):
    return x * _sigmoid(x)


def _gelu_tanh(x):
    c = 0.7978845608028654
    return 0.5 * x * (1.0 + jnp.tanh(c * (x + 0.044715 * (x * x * x))))


def _rmsnorm(x, g):
    ms = jnp.mean(x * x, axis=-1, keepdims=True)
    return x * lax.rsqrt(ms + EPS) * g


def _row_bcast(x, period, r):
    n, w = x.shape
    x3 = x.reshape(n // period, period, w)
    return jnp.broadcast_to(x3[:, r:r + 1, :], x3.shape).reshape(n, w)


def _hgrn_head(qh, kh, vh, bh, st_ref, hd):
    c = qh.shape[0]
    row = lax.broadcasted_iota(jnp.int32, (c, HG_DIM), 0)
    rr = lax.broadcasted_iota(jnp.int32, (c, c), 0)
    cc = lax.broadcasted_iota(jnp.int32, (c, c), 1)
    a = jnp.zeros((c, c), F32)

    half = c // 2
    while half >= DIAG:
        blk = 2 * half
        mid = _row_bcast(bh, blk, half - 1)
        upper = (row & (blk - 1)) >= half
        e = jnp.where(upper, bh - mid, mid - bh)
        y = (jnp.where(upper, qh, kh) * jnp.exp(e)).astype(BF16)
        g = _dot_nt(y, y)
        same = (rr & (-blk)) == (cc & (-blk))
        take = same & ((rr & (blk - 1)) >= half) & ((cc & (blk - 1)) < half)
        a = jnp.where(take, g, a)
        half //= 2

    delta = cc - (rr & (-DIAG))
    for s in range(DIAG):
        bs = _row_bcast(bh, DIAG, s)
        ks = _row_bcast(kh, DIAG, s)
        p = qh * jnp.exp(jnp.minimum(bh - bs, 0.0)) * ks
        col = jnp.sum(p, axis=-1, keepdims=True)
        a = jnp.where(delta == s, col, a)
    a = jnp.where(cc <= rr, a, 0.0)

    st = st_ref[hd]
    qb = (qh * jnp.exp(bh)).astype(BF16)
    vb = vh.astype(BF16)
    o = _dot(a.astype(BF16), vb) + _dot_nt(qb, st.astype(BF16))
    b_last = bh[c - 1:c, :]
    kb = (kh * jnp.exp(b_last - bh)).astype(BF16)
    st_ref[hd] = st * jnp.exp(b_last) + _dot_tn(vb, kb)
    return o


def _split3(x):
    hi = x.astype(BF16)
    r1 = x - hi.astype(F32)
    mid = r1.astype(BF16)
    lo = (r1 - mid.astype(F32)).astype(BF16)
    return hi, mid, lo


def _hgrn_chunk(q, k, v, log_f, st_ref):
    c = q.shape[0]
    rr = lax.broadcasted_iota(jnp.int32, (c, c), 0)
    cc = lax.broadcasted_iota(jnp.int32, (c, c), 1)
    tri = jnp.where(cc <= rr, 1.0, 0.0).astype(BF16)
    hi, mid, lo = _split3(log_f)
    b = _dot(tri, hi) + _dot(tri, mid) + _dot(tri, lo)
    outs = []
    for hd in range(HG_HEADS):
        sl = slice(hd * HG_DIM, (hd + 1) * HG_DIM)
        outs.append(_hgrn_head(q[:, sl], k[:, sl], v[:, sl], b[:, sl], st_ref, hd))
    return jnp.concatenate(outs, axis=-1)


def _ret_chunk(q, k, v, dmat_ref, wq, ws, cdec, r_ref):
    c = q.shape[0]
    lane_q = lax.broadcasted_iota(jnp.int32, (c, RET_QK), 1)
    lane_v = lax.broadcasted_iota(jnp.int32, (c, RET_V), 1)
    kb = k.astype(BF16)
    vb = v.astype(BF16)
    out = jnp.zeros((c, RET_V), F32)
    for hd in range(RET_HEADS):
        qm = jnp.where((lane_q // RET_DK) == hd, q, 0.0).astype(BF16)
        sc = (_dot_nt(qm, kb) * dmat_ref[hd]).astype(BF16)
        out = jnp.where((lane_v // RET_DV) == hd, _dot(sc, vb), out)
    r_prev = r_ref[...]
    out = out + _dot((q * wq).astype(BF16), r_prev.astype(BF16))
    upd = _dot_tn((k * ws).astype(BF16), vb)
    rs = lax.broadcasted_iota(jnp.int32, (RET_QK, RET_V), 0)
    cs = lax.broadcasted_iota(jnp.int32, (RET_QK, RET_V), 1)
    r_ref[...] = r_prev * cdec + jnp.where((rs // RET_DK) == (cs // RET_DV), upd, 0.0)
    return out


def _groupnorm64(o, g):
    parts = []
    for tile in range(RET_V // 128):
        x = o[:, tile * 128:(tile + 1) * 128]
        lo = lax.broadcasted_iota(jnp.int32, x.shape, 1) < RET_DV

        def seg_mean(z):
            s_lo = jnp.sum(jnp.where(lo, z, 0.0), axis=-1, keepdims=True)
            s_hi = jnp.sum(jnp.where(lo, 0.0, z), axis=-1, keepdims=True)
            return jnp.where(lo, s_lo, s_hi) * (1.0 / RET_DV)

        d = x - seg_mean(x)
        var = seg_mean(d * d)
        parts.append(d * lax.rsqrt(var + EPS))
    return jnp.concatenate(parts, axis=-1) * g


def _lru_scan(a, u, h0):
    n = a.shape[0]
    row = lax.broadcasted_iota(jnp.int32, a.shape, 0)
    sh = 1
    while sh < n:
        a_s = pltpu.roll(a, sh, 0)
        u_s = pltpu.roll(u, sh, 0)
        valid = row >= sh
        u = jnp.where(valid, a * u_s + u, u)
        a = jnp.where(valid, a * a_s, a)
        sh *= 2
    return u + a * h0


def _mixer_block(rows, chunk, first_layer, h_ref, rope_ref, dmat_ref, rtab_ref, cdec_ref, o_ref,
                 ng_ref, win_ref, lbrow_ref, hgn_ref, rtn_ref, cw_ref, cb_ref, wg_ref, gb_ref,
                 lam_ref, wout_ref, st_ref, r_ref, lruh_ref, hist_ref):
    h = h_ref[...]
    hn = _rmsnorm(h, ng_ref[...]).astype(BF16)

    p = _dot(hn, win_ref[:, 0:HG_COLS])
    hq = p[:, 0:HG_WIDTH]
    z = p[:, HG_WIDTH:2 * HG_WIDTH]
    hv = p[:, 2 * HG_WIDTH:3 * HG_WIDTH]
    hg = p[:, 3 * HG_WIDTH:4 * HG_WIDTH]
    ls = jnp.minimum(z, 0.0) - jnp.log1p(jnp.exp(-jnp.abs(z)))
    k_hg = jnp.exp(ls - z)
    if first_layer:
        log_f = ls
    else:
        log_lb = lbrow_ref[0:1, :]
        cpre = lbrow_ref[1:2, :] + ls
        log_f = jnp.maximum(log_lb, cpre) + jnp.log1p(jnp.exp(-jnp.abs(log_lb - cpre)))
        k_hg = lbrow_ref[2:3, :] * k_hg
    q_hg = _silu(hq) * (HG_DIM ** -0.5)

    p = _dot(hn, win_ref[:, HG_COLS:HG_COLS + RET_COLS])
    cos = rope_ref[:, 0:RET_QK]
    sin_a = rope_ref[:, RET_QK:2 * RET_QK]
    sin_b = rope_ref[:, 2 * RET_QK:3 * RET_QK]

    def rope(x):
        return (x * cos + pltpu.roll(x, RET_QK - RET_DK // 2, 1) * sin_a
                + pltpu.roll(x, RET_DK // 2, 1) * sin_b)

    q_r = rope(p[:, 0:RET_QK])
    k_r = rope(p[:, RET_QK:2 * RET_QK]) * (RET_DK ** -0.5)
    v_r = p[:, 2 * RET_QK:2 * RET_QK + RET_V]
    g_r = p[:, 2 * RET_QK + RET_V:RET_COLS]

    o_hg, o_rt = [], []
    for ci in range(rows // chunk):
        rs = slice(ci * chunk, (ci + 1) * chunk)
        o_hg.append(_hgrn_chunk(q_hg[rs], k_hg[rs], hv[rs], log_f[rs], st_ref))
        o_rt.append(_ret_chunk(q_r[rs], k_r[rs], v_r[rs], dmat_ref, rtab_ref[:, 0:RET_QK],
                               rtab_ref[:, RET_QK:2 * RET_QK], cdec_ref[...], r_ref))
    o_hg = jnp.concatenate(o_hg, axis=0) if len(o_hg) > 1 else o_hg[0]
    o_rt = jnp.concatenate(o_rt, axis=0) if len(o_rt) > 1 else o_rt[0]

    heads = []
    for hd in range(HG_HEADS):
        x = o_hg[:, hd * HG_DIM:(hd + 1) * HG_DIM]
        heads.append(x * lax.rsqrt(jnp.mean(x * x, axis=-1, keepdims=True) + EPS))
    m_hg = jnp.concatenate(heads, axis=-1) * hgn_ref[...] * _silu(hg)
    m_rt = _groupnorm64(o_rt, rtn_ref[...]) * _silu(g_r)

    p = _dot(hn, win_ref[:, HG_COLS + RET_COLS:IN_COLS])
    lx = p[:, 0:LRU_W]
    ly = p[:, LRU_W:LRU_COLS]
    hist_ref[CONV_PAD:CONV_PAD + rows, :] = lx
    xc = cb_ref[...]
    for tap in range(LRU_CONV):
        off = CONV_PAD - (LRU_CONV - 1) + tap
        xc = xc + cw_ref[tap:tap + 1, :] * hist_ref[off:off + rows, :]
    hist_ref[0:CONV_PAD, :] = hist_ref[rows:rows + CONV_PAD, :]
    gates = _dot(xc.astype(BF16), wg_ref[...]) + gb_ref[...]
    r_gate = _sigmoid(gates[:, 0:LRU_W])
    i_gate = _sigmoid(gates[:, LRU_W:2 * LRU_W])
    lam = lam_ref[...]
    softplus_neg = jnp.maximum(-lam, 0.0) + jnp.log1p(jnp.exp(-jnp.abs(lam)))
    log_a = (-LRU_C) * r_gate * softplus_neg
    a = jnp.exp(log_a)
    u = jnp.sqrt(-jnp.tanh(log_a) * (a * a + 1.0)) * (i_gate * xc)
    hseq = _lru_scan(a, u, lruh_ref[0:1, :])
    lruh_ref[0:1, :] = hseq[rows - 1:rows, :]
    m_lr = hseq * _gelu_tanh(ly)

    out = (h + _dot(m_hg.astype(BF16), wout_ref[0:HG_WIDTH, :])
           + _dot(m_rt.astype(BF16), wout_ref[HG_WIDTH:HG_WIDTH + RET_V, :])
           + _dot(m_lr.astype(BF16), wout_ref[HG_WIDTH + RET_V:D_MODEL, :]))
    o_ref[...] = out


def _mixer_kernel(layer, hm_ref, hx_ref, rope_m_ref, rope_x_ref, dmat_m_ref, dmat_x_ref,
                  rtab_m_ref, rtab_x_ref, cdec_m_ref, cdec_x_ref, ng_ref, win_ref, lbl_ref,
                  hgn_ref, rtn_ref, cw_ref, cb_ref, wg_ref, gb_ref, lam_ref, wout_ref,
                  om_ref, ox_ref, st_ref, r_ref, lruh_ref, hist_ref, lbrow_ref):
    step = pl.program_id(1)
    shared = (ng_ref, win_ref, lbrow_ref, hgn_ref, rtn_ref, cw_ref, cb_ref, wg_ref, gb_ref,
              lam_ref, wout_ref, st_ref, r_ref, lruh_ref, hist_ref)

    @pl.when(step == 0)
    def _():
        st_ref[...] = jnp.zeros_like(st_ref)
        r_ref[...] = jnp.zeros_like(r_ref)
        lruh_ref[...] = jnp.zeros_like(lruh_ref)
        hist_ref[0:CONV_PAD, :] = jnp.zeros((CONV_PAD, LRU_W), F32)
        if layer > 0:
            lg = lbl_ref[...]
            ex = jnp.exp(lg - jnp.max(lg, axis=0, keepdims=True))
            inv = 1.0 / jnp.sum(ex, axis=0, keepdims=True)
            lb = sum(ex[j:j + 1, :] for j in range(1, layer + 1)) * inv
            rest = sum(ex[j:j + 1, :] for j in range(ex.shape[0]) if j == 0 or j > layer) * inv
            lbrow_ref[0:1, :] = jnp.log(lb)
            lbrow_ref[1:2, :] = jnp.log(rest)
            lbrow_ref[2:3, :] = rest
        _mixer_block(N_META, N_META, layer == 0, hm_ref, rope_m_ref, dmat_m_ref, rtab_m_ref,
                     cdec_m_ref, om_ref, *shared)

    @pl.when(step > 0)
    def _():
        rows = hx_ref.shape[0]
        _mixer_block(rows, min(CHUNK, rows), layer == 0, hx_ref, rope_x_ref, dmat_x_ref,
                     rtab_x_ref, cdec_x_ref, ox_ref, *shared)


def _const_spec(shape):
    nd = len(shape)
    return pl.BlockSpec(shape, lambda b, t, _n=nd: (0,) * _n, pipeline_mode=pl.Buffered(1))


def _layer_spec(shape):
    nd = len(shape)

    def make(layer):
        return pl.BlockSpec((None,) + tuple(shape), lambda b, t, _l=layer, _n=nd: (_l,) + (0,) * _n,
                            pipeline_mode=pl.Buffered(1))
    return make


def _row_specs(tb):
    meta = pl.BlockSpec((None, N_META, D_MODEL), lambda b, t: (b, 0, 0))
    main = pl.BlockSpec((None, tb, D_MODEL), lambda b, t: (b, jnp.maximum(t - 1, 0), 0))
    return meta, main


def _mixer_call(layer, batch, seq, tb, chunk):
    nt = seq // tb
    meta_spec, main_spec = _row_specs(tb)
    in_specs = [
        meta_spec, main_spec,
        _const_spec((N_META, 3 * RET_QK)),
        pl.BlockSpec((tb, 3 * RET_QK), lambda b, t: (jnp.maximum(t - 1, 0), 0)),
        _const_spec((RET_HEADS, N_META, N_META)), _const_spec((RET_HEADS, chunk, chunk)),
        _const_spec((N_META, 2 * RET_QK)), _const_spec((chunk, 2 * RET_QK)),
        _const_spec((1, RET_V)), _const_spec((1, RET_V)),
        _layer_spec((1, D_MODEL))(layer),
        _layer_spec((D_MODEL, IN_COLS))(layer),
        _const_spec((4, HG_WIDTH)),
        _layer_spec((1, HG_WIDTH))(layer),
        _layer_spec((1, RET_V))(layer),
        _layer_spec((LRU_CONV, LRU_W))(layer),
        _layer_spec((1, LRU_W))(layer),
        _layer_spec((LRU_W, 2 * LRU_W))(layer),
        _layer_spec((1, 2 * LRU_W))(layer),
        _layer_spec((1, LRU_W))(layer),
        _layer_spec((D_MODEL, D_MODEL))(layer),
    ]
    return pl.pallas_call(
        functools.partial(_mixer_kernel, layer),
        grid=(batch, nt + 1),
        in_specs=in_specs,
        out_specs=list(_row_specs(tb)),
        out_shape=[jax.ShapeDtypeStruct((batch, N_META, D_MODEL), F32),
                   jax.ShapeDtypeStruct((batch, seq, D_MODEL), F32)],
        scratch_shapes=[
            pltpu.VMEM((HG_HEADS, HG_DIM, HG_DIM), F32),
            pltpu.VMEM((RET_QK, RET_V), F32),
            pltpu.VMEM((SUBLANES, LRU_W), F32),
            pltpu.VMEM((CONV_PAD + tb, LRU_W), F32),
            pltpu.VMEM((SUBLANES, HG_WIDTH), F32),
        ],
        compiler_params=pltpu.CompilerParams(
            dimension_semantics=("arbitrary", "arbitrary"), vmem_limit_bytes=VMEM_LIMIT),
        name=f"mixer_l{layer}",
    )


def _ffn_block(rows, last, h_ref, o_ref, ng_ref, wup_ref, cw_ref, cb_ref, wdn_ref, fg_ref,
               tail_ref, buf_ref):
    h = h_ref[...]
    hn = _rmsnorm(h, ng_ref[...]).astype(BF16)
    acc = h
    for j in range(N_SLABS):
        cols = slice(j * 2 * FFN_COLS, (j + 1) * 2 * FFN_COLS)
        slot = j % 2
        buf_ref[slot, 0:CONV_PAD, :] = tail_ref[:, cols]
        buf_ref[slot, CONV_PAD:CONV_PAD + rows, :] = _dot(hn, wup_ref[:, cols])
        u = cb_ref[:, cols]
        for tap in range(FFN_CONV):
            off = CONV_PAD - (FFN_CONV - 1) + tap
            u = u + cw_ref[tap:tap + 1, cols] * buf_ref[slot, off:off + rows, :]
        tail_ref[:, cols] = buf_ref[slot, rows:rows + CONV_PAD, :]
        act = (_silu(u[:, 0:FFN_COLS]) * u[:, FFN_COLS:2 * FFN_COLS]).astype(BF16)
        acc = acc + _dot(act, wdn_ref[j * FFN_COLS:(j + 1) * FFN_COLS, :])
    if last:
        acc = _rmsnorm(acc, fg_ref[...])
    o_ref[...] = acc


def _ffn_kernel(last, hm_ref, hx_ref, ng_ref, wup_ref, cw_ref, cb_ref, wdn_ref, fg_ref,
                om_ref, ox_ref, tail_ref, buf_ref):
    step = pl.program_id(1)
    shared = (ng_ref, wup_ref, cw_ref, cb_ref, wdn_ref, fg_ref, tail_ref, buf_ref)

    @pl.when(step == 0)
    def _():
        tail_ref[...] = jnp.zeros_like(tail_ref)
        _ffn_block(N_META, last, hm_ref, om_ref, *shared)

    @pl.when(step > 0)
    def _():
        _ffn_block(hx_ref.shape[0], last, hx_ref, ox_ref, *shared)


def _ffn_call(layer, last, batch, seq, tb):
    nt = seq // tb
    meta_spec, main_spec = _row_specs(tb)
    in_specs = [
        meta_spec, main_spec,
        _layer_spec((1, D_MODEL))(layer),
        _layer_spec((D_MODEL, 2 * D_FF))(layer),
        _layer_spec((FFN_CONV, 2 * D_FF))(layer),
        _layer_spec((1, 2 * D_FF))(layer),
        _layer_spec((D_FF, D_MODEL))(layer),
        _const_spec((1, D_MODEL)),
    ]
    return pl.pallas_call(
        functools.partial(_ffn_kernel, last),
        grid=(batch, nt + 1),
        in_specs=in_specs,
        out_specs=list(_row_specs(tb)),
        out_shape=[jax.ShapeDtypeStruct((batch, N_META, D_MODEL), F32),
                   jax.ShapeDtypeStruct((batch, seq, D_MODEL), F32)],
        scratch_shapes=[
            pltpu.VMEM((CONV_PAD, 2 * D_FF), F32),
            pltpu.VMEM((2, CONV_PAD + tb, 2 * FFN_COLS), F32),
        ],
        compiler_params=pltpu.CompilerParams(
            dimension_semantics=("arbitrary", "arbitrary"), vmem_limit_bytes=VMEM_LIMIT),
        name=f"ffn_l{layer}",
    )


def _rope_table(pos):
    theta = ROPE_BASE ** (-jnp.linspace(0.0, 1.0, RET_DK // 2, dtype=F32))
    ang = pos[:, None] * theta[None, :]
    cos, sin = jnp.cos(ang), jnp.sin(ang)
    zero = jnp.zeros_like(sin)
    cos_t = jnp.tile(jnp.concatenate([cos, cos], axis=-1), (1, RET_HEADS))
    sin_a = jnp.tile(jnp.concatenate([-sin, zero], axis=-1), (1, RET_HEADS))
    sin_b = jnp.tile(jnp.concatenate([zero, sin], axis=-1), (1, RET_HEADS))
    return jnp.concatenate([cos_t, sin_a, sin_b], axis=-1)


def _retention_tables(c):
    log_gamma = jnp.log1p(-jnp.exp2(-5.0 - jnp.arange(RET_HEADS, dtype=F32)))
    idx = jnp.arange(c, dtype=F32)
    lg = log_gamma[:, None]
    rel = idx[:, None] - idx[None, :]
    dmat = jnp.where(rel >= 0, jnp.exp(lg[:, :, None] * jnp.maximum(rel, 0.0)), 0.0)
    w_q = jnp.exp(lg * (idx + 1.0))
    w_s = jnp.exp(lg * (c - 1 - idx))
    rtab = jnp.concatenate([jnp.repeat(w_q.T, RET_DK, axis=1), jnp.repeat(w_s.T, RET_DK, axis=1)], axis=-1)
    cdec = jnp.repeat(jnp.exp(log_gamma * c), RET_DV)[None, :]
    return dmat, rtab, cdec


def _block_diag(w):
    depth, n, d, _ = w.shape
    eye = jnp.eye(n, dtype=w.dtype)
    return jnp.einsum('lnde,nm->lndme', w, eye).reshape(depth, n * d, n * d)


def _slab_order(a):
    lead = a.shape[:-1]
    return a.reshape(lead + (2, N_SLABS, FFN_COLS)).swapaxes(-3, -2).reshape(lead + (2 * D_FF,))


def kernel(x, meta_tokens, norm_mix_g, w_in, hg_lb_logits, hg_norm_g, ret_norm_g, lru_conv_w, lru_conv_b, lru_gate_a_w, lru_gate_a_b, lru_gate_i_w, lru_gate_i_b, lru_lambda, w_out, norm_ffn_g, ffn_w_up, ffn_conv_w, ffn_conv_b, ffn_w_down, final_norm_g):
    batch, seq, d_model = x.shape
    depth = w_in.shape[0]
    assert d_model == D_MODEL and w_in.shape[2] == IN_COLS and hg_lb_logits.shape == (4, HG_WIDTH)
    tb = min(SEQ_BLOCK, seq)
    chunk = min(CHUNK, tb)
    assert seq % tb == 0 and tb % chunk == 0 and chunk % DIAG == 0

    rope_m = _rope_table(jnp.arange(N_META, dtype=F32))
    rope_x = _rope_table(jnp.arange(N_META, N_META + seq, dtype=F32))
    dmat_m, rtab_m, cdec_m = _retention_tables(N_META)
    dmat_x, rtab_x, cdec_x = _retention_tables(chunk)

    w_in_b = w_in.astype(BF16)
    w_out_b = w_out.astype(BF16)
    w_gate = jnp.concatenate([_block_diag(lru_gate_a_w), _block_diag(lru_gate_i_w)], axis=-1).astype(BF16)
    gate_b = jnp.concatenate([lru_gate_a_b, lru_gate_i_b], axis=-1)[:, None, :]
    w_up_b = _slab_order(ffn_w_up).astype(BF16)
    conv_w = _slab_order(ffn_conv_w)
    conv_b = _slab_order(ffn_conv_b)[:, None, :]
    w_dn_b = ffn_w_down.astype(BF16)
    hgn = jnp.tile(hg_norm_g, (1, HG_HEADS))[:, None, :]
    rtn = jnp.tile(ret_norm_g, (1, RET_HEADS))[:, None, :]

    hm = jnp.broadcast_to(meta_tokens[None].astype(x.dtype), (batch, N_META, d_model))
    hx = x
    for layer in range(depth):
        hm, hx = _mixer_call(layer, batch, seq, tb, chunk)(
            hm, hx, rope_m, rope_x, dmat_m, dmat_x, rtab_m, rtab_x, cdec_m, cdec_x,
            norm_mix_g[:, None, :], w_in_b, hg_lb_logits, hgn, rtn, lru_conv_w,
            lru_conv_b[:, None, :], w_gate, gate_b, lru_lambda[:, None, :], w_out_b)
        hm, hx = _ffn_call(layer, layer == depth - 1, batch, seq, tb)(
            hm, hx, norm_ffn_g[:, None, :], w_up_b, conv_w, conv_b, w_dn_b, final_norm_g[None, :])
    return hx
```

```python
import functools

import jax
import jax.numpy as jnp
from jax import lax
from jax.experimental import pallas as pl
from jax.experimental.pallas import tpu as pltpu

F32 = jnp.float32
BF16 = jnp.bfloat16

D_MODEL = 1024
N_META = 16
EPS = 1e-6
HG_HEADS = 4
HG_DIM = 128
HG_WIDTH = HG_HEADS * HG_DIM
RET_HEADS = 4
RET_DK = 32
RET_DV = 64
RET_QK = RET_HEADS * RET_DK
RET_V = RET_HEADS * RET_DV
ROPE_BASE = 10000.0
LRU_W = 256
LRU_BLOCKS = 4
LRU_CONV = 4
LRU_C = 8.0
D_FF = 2816
FFN_CONV = 3
HG_COLS = 4 * HG_WIDTH
RET_COLS = 2 * RET_QK + 2 * RET_V
LRU_COLS = 2 * LRU_W
IN_COLS = HG_COLS + RET_COLS + LRU_COLS

SUBLANES = 8
DIAG = SUBLANES
SEQ_BLOCK = 256
CHUNK = 128
FFN_COLS = 256
N_SLABS = D_FF // FFN_COLS
CONV_PAD = SUBLANES
VMEM_LIMIT = 56 * 1024 * 1024
LOG2E = 1.4426950408889634


def _dot(a, b):
    return jnp.dot(a, b, preferred_element_type=F32)


def _dot_nt(a, b):
    return lax.dot_general(a, b, (((1,), (1,)), ((), ())), preferred_element_type=F32)


def _dot_tn(a, b):
    return lax.dot_general(a, b, (((0,), (0,)), ((), ())), preferred_element_type=F32)


def _sigmoid(x):
    return 0.5 * jnp.tanh(0.5 * x) + 0.5


def _silu(x):
    hx = 0.5 * x
    return hx + hx * jnp.tanh(hx)


def _gelu_tanh(x):
    c = 0.7978845608028654
    return 0.5 * x * (1.0 + jnp.tanh(c * (x + 0.044715 * (x * x * x))))


def _rmsnorm(x, g):
    ms = jnp.mean(x * x, axis=-1, keepdims=True)
    return x * lax.rsqrt(ms + EPS) * g


def _bcast_rows(ref, base, cols, n, period, r):
    pieces = [jnp.broadcast_to(ref[base + g + r:base + g + r + 1, cols], (period, HG_DIM))
              for g in range(0, n, period)]
    return pieces[0] if len(pieces) == 1 else jnp.concatenate(pieces, axis=0)


def _hgrn_masks(c):
    rr = lax.broadcasted_iota(jnp.int32, (c, c), 0)
    cc = lax.broadcasted_iota(jnp.int32, (c, c), 1)
    row = lax.broadcasted_iota(jnp.int32, (c, HG_DIM), 0)
    levels = []
    half = c // 2
    while half >= DIAG:
        blk = 2 * half
        same = (rr & (-blk)) == (cc & (-blk))
        take = same & ((rr & (blk - 1)) >= half) & ((cc & (blk - 1)) < half)
        levels.append((half, (row & (blk - 1)) >= half, take))
        half //= 2
    delta = cc - (rr & (-DIAG))
    diag = [delta == s for s in range(DIAG)]
    return levels, diag, cc <= rr


def _hgrn_head(qh, vh, bh, blh, b_ref, bl_ref, base, cols, masks, st_ref, hd):
    c = qh.shape[0]
    levels, diag, tri = masks
    a = jnp.zeros((c, c), F32)
    for half, upper, take in levels:
        mid = _bcast_rows(b_ref, base, cols, c, 2 * half, half - 1)
        ex = jnp.exp2(jnp.where(upper, bh - mid, mid - blh))
        y = jnp.where(upper, qh * ex, ex).astype(BF16)
        a = jnp.where(take, _dot_nt(y, y), a)
    for s in range(DIAG):
        p = qh * jnp.exp2(bh - _bcast_rows(bl_ref, base, cols, c, DIAG, s))
        a = jnp.where(diag[s], jnp.sum(p, axis=-1, keepdims=True), a)
    a = jnp.where(tri, a, 0.0)

    st = st_ref[hd]
    qb = (qh * jnp.exp2(bh)).astype(BF16)
    vb = vh.astype(BF16)
    o = _dot(a.astype(BF16), vb) + _dot_nt(qb, st.astype(BF16))
    b_last = bh[c - 1:c, :]
    kb = jnp.exp2(b_last - blh).astype(BF16)
    st_ref[hd] = st * jnp.exp2(b_last) + _dot_tn(vb, kb)
    return o


def _split3(x):
    hi = x.astype(BF16)
    r1 = x - hi.astype(F32)
    mid = r1.astype(BF16)
    lo = (r1 - mid.astype(F32)).astype(BF16)
    return hi, mid, lo


def _hgrn_chunk(q, log2_k, v, log2_f, b_ref, bl_ref, base, masks, st_ref):
    c = q.shape[0]
    hi, mid, lo = _split3(log2_f)
    tri = jnp.where(masks[2], 1.0, 0.0).astype(BF16)
    b = _dot(tri, hi) + _dot(tri, mid) + _dot(tri, lo)
    bl = b - log2_k
    b_ref[base:base + c, :] = b
    bl_ref[base:base + c, :] = bl
    outs = []
    for hd in range(HG_HEADS):
        sl = slice(hd * HG_DIM, (hd + 1) * HG_DIM)
        outs.append(_hgrn_head(q[:, sl], v[:, sl], b[:, sl], bl[:, sl], b_ref, bl_ref, base, sl,
                               masks, st_ref, hd))
    return jnp.concatenate(outs, axis=-1)


def _ret_chunk(q, k, v, dmat_ref, wq, ws, cdec, r_ref):
    c = q.shape[0]
    lane_q = lax.broadcasted_iota(jnp.int32, (c, RET_QK), 1)
    lane_v = lax.broadcasted_iota(jnp.int32, (c, RET_V), 1)
    kb = k.astype(BF16)
    vb = v.astype(BF16)
    out = jnp.zeros((c, RET_V), F32)
    for hd in range(RET_HEADS):
        qm = jnp.where((lane_q // RET_DK) == hd, q, 0.0).astype(BF16)
        sc = (_dot_nt(qm, kb) * dmat_ref[hd]).astype(BF16)
        out = jnp.where((lane_v // RET_DV) == hd, _dot(sc, vb), out)
    r_prev = r_ref[...]
    out = out + _dot((q * wq).astype(BF16), r_prev.astype(BF16))
    upd = _dot_tn((k * ws).astype(BF16), vb)
    rs = lax.broadcasted_iota(jnp.int32, (RET_QK, RET_V), 0)
    cs = lax.broadcasted_iota(jnp.int32, (RET_QK, RET_V), 1)
    r_ref[...] = r_prev * cdec + jnp.where((rs // RET_DK) == (cs // RET_DV), upd, 0.0)
    return out


def _groupnorm64(o, g):
    parts = []
    for tile in range(RET_V // 128):
        x = o[:, tile * 128:(tile + 1) * 128]
        lo = lax.broadcasted_iota(jnp.int32, x.shape, 1) < RET_DV

        def seg_mean(z):
            s_lo = jnp.sum(jnp.where(lo, z, 0.0), axis=-1, keepdims=True)
            s_hi = jnp.sum(jnp.where(lo, 0.0, z), axis=-1, keepdims=True)
            return jnp.where(lo, s_lo, s_hi) * (1.0 / RET_DV)

        d = x - seg_mean(x)
        var = seg_mean(d * d)
        parts.append(d * lax.rsqrt(var + EPS))
    return jnp.concatenate(parts, axis=-1) * g


def _lru_scan(a, u, h0):
    n = a.shape[0]
    row = lax.broadcasted_iota(jnp.int32, a.shape, 0)
    sh = 1
    while sh < n:
        a_s = pltpu.roll(a, sh, 0)
        u_s = pltpu.roll(u, sh, 0)
        valid = row >= sh
        u = jnp.where(valid, a * u_s + u, u)
        a = jnp.where(valid, a * a_s, a)
        sh *= 2
    return u + a * h0


def _mixer_block(rows, chunk, first_layer, h_ref, rope_ref, dmat_ref, rtab_ref, cdec_ref, o_ref,
                 ng_ref, win_ref, lbrow_ref, hgn_ref, rtn_ref, cw_ref, cb_ref, wg_ref, gb_ref,
                 lam_ref, wout_ref, st_ref, r_ref, lruh_ref, hist_ref, b_ref, bl_ref):
    h = h_ref[...]
    hn = _rmsnorm(h, ng_ref[...]).astype(BF16)

    p = _dot(hn, win_ref[:, 0:HG_COLS])
    hq = p[:, 0:HG_WIDTH]
    z = p[:, HG_WIDTH:2 * HG_WIDTH]
    hv = p[:, 2 * HG_WIDTH:3 * HG_WIDTH]
    hg = p[:, 3 * HG_WIDTH:4 * HG_WIDTH]
    ls = jnp.minimum(z, 0.0) - jnp.log(1.0 + jnp.exp(-jnp.abs(z)))
    log_k = ls - z
    if first_layer:
        log_f = ls
    else:
        lb = lbrow_ref[0:1, :]
        log_f = jnp.where(lb > 0.0, jnp.log(lb + lbrow_ref[1:2, :] * jnp.exp(ls)), ls)
        log_k = log_k + lbrow_ref[2:3, :]
    log2_f = log_f * LOG2E
    log2_k = log_k * LOG2E
    q_hg = (hq * (0.5 * HG_DIM ** -0.5)) * (1.0 + jnp.tanh(0.5 * hq))

    p = _dot(hn, win_ref[:, HG_COLS:HG_COLS + RET_COLS])
    cos = rope_ref[:, 0:RET_QK]
    sin_a = rope_ref[:, RET_QK:2 * RET_QK]
    sin_b = rope_ref[:, 2 * RET_QK:3 * RET_QK]

    def rope(x):
        return (x * cos + pltpu.roll(x, RET_QK - RET_DK // 2, 1) * sin_a
                + pltpu.roll(x, RET_DK // 2, 1) * sin_b)

    q_r = rope(p[:, 0:RET_QK])
    k_r = rope(p[:, RET_QK:2 * RET_QK]) * (RET_DK ** -0.5)
    v_r = p[:, 2 * RET_QK:2 * RET_QK + RET_V]
    g_r = p[:, 2 * RET_QK + RET_V:RET_COLS]

    o_hg, o_rt = [], []
    masks = _hgrn_masks(chunk)
    for ci in range(rows // chunk):
        rs = slice(ci * chunk, (ci + 1) * chunk)
        o_hg.append(_hgrn_chunk(q_hg[rs], log2_k[rs], hv[rs], log2_f[rs], b_ref, bl_ref,
                                ci * chunk, masks, st_ref))
        o_rt.append(_ret_chunk(q_r[rs], k_r[rs], v_r[rs], dmat_ref, rtab_ref[:, 0:RET_QK],
                               rtab_ref[:, RET_QK:2 * RET_QK], cdec_ref[...], r_ref))
    o_hg = jnp.concatenate(o_hg, axis=0) if len(o_hg) > 1 else o_hg[0]
    o_rt = jnp.concatenate(o_rt, axis=0) if len(o_rt) > 1 else o_rt[0]

    heads = []
    for hd in range(HG_HEADS):
        x = o_hg[:, hd * HG_DIM:(hd + 1) * HG_DIM]
        heads.append(x * lax.rsqrt(jnp.mean(x * x, axis=-1, keepdims=True) + EPS))
    m_hg = jnp.concatenate(heads, axis=-1) * hgn_ref[...] * _silu(hg)
    m_rt = _groupnorm64(o_rt, rtn_ref[...]) * _silu(g_r)

    p = _dot(hn, win_ref[:, HG_COLS + RET_COLS:IN_COLS])
    lx = p[:, 0:LRU_W]
    ly = p[:, LRU_W:LRU_COLS]
    hist_ref[CONV_PAD:CONV_PAD + rows, :] = lx
    xc = cb_ref[...]
    for tap in range(LRU_CONV):
        off = CONV_PAD - (LRU_CONV - 1) + tap
        xc = xc + cw_ref[tap:tap + 1, :] * hist_ref[off:off + rows, :]
    hist_ref[0:CONV_PAD, :] = hist_ref[rows:rows + CONV_PAD, :]
    gates = _dot(xc.astype(BF16), wg_ref[...]) + gb_ref[...]
    r_gate = _sigmoid(gates[:, 0:LRU_W])
    i_gate = _sigmoid(gates[:, LRU_W:2 * LRU_W])
    lam = lam_ref[...]
    softplus_neg = jnp.maximum(-lam, 0.0) + jnp.log(1.0 + jnp.exp(-jnp.abs(lam)))
    log_a = (-LRU_C) * r_gate * softplus_neg
    a = jnp.exp(log_a)
    u = jnp.sqrt(-jnp.tanh(log_a) * (a * a + 1.0)) * (i_gate * xc)
    hseq = _lru_scan(a, u, lruh_ref[0:1, :])
    lruh_ref[0:1, :] = hseq[rows - 1:rows, :]
    m_lr = hseq * _gelu_tanh(ly)

    out = (h + _dot(m_hg.astype(BF16), wout_ref[0:HG_WIDTH, :])
           + _dot(m_rt.astype(BF16), wout_ref[HG_WIDTH:HG_WIDTH + RET_V, :])
           + _dot(m_lr.astype(BF16), wout_ref[HG_WIDTH + RET_V:D_MODEL, :]))
    o_ref[...] = out


def _mixer_kernel(layer, hm_ref, hx_ref, rope_m_ref, rope_x_ref, dmat_m_ref, dmat_x_ref,
                  rtab_m_ref, rtab_x_ref, cdec_m_ref, cdec_x_ref, ng_ref, win_ref, lbl_ref,
                  hgn_ref, rtn_ref, cw_ref, cb_ref, wg_ref, gb_ref, lam_ref, wout_ref,
                  om_ref, ox_ref, st_ref, r_ref, lruh_ref, hist_ref, lbrow_ref, b_ref, bl_ref):
    step = pl.program_id(1)
    shared = (ng_ref, win_ref, lbrow_ref, hgn_ref, rtn_ref, cw_ref, cb_ref, wg_ref, gb_ref,
              lam_ref, wout_ref, st_ref, r_ref, lruh_ref, hist_ref, b_ref, bl_ref)

    @pl.when(step == 0)
    def _():
        st_ref[...] = jnp.zeros_like(st_ref)
        r_ref[...] = jnp.zeros_like(r_ref)
        lruh_ref[...] = jnp.zeros_like(lruh_ref)
        hist_ref[0:CONV_PAD, :] = jnp.zeros((CONV_PAD, LRU_W), F32)
        if layer > 0:
            lg = lbl_ref[...]
            ex = jnp.exp(lg - jnp.max(lg, axis=0, keepdims=True))
            inv = 1.0 / jnp.sum(ex, axis=0, keepdims=True)
            lb = sum(ex[j:j + 1, :] for j in range(1, layer + 1)) * inv
            rest = sum(ex[j:j + 1, :] for j in range(ex.shape[0]) if j == 0 or j > layer) * inv
            lbrow_ref[0:1, :] = lb
            lbrow_ref[1:2, :] = rest
            lbrow_ref[2:3, :] = jnp.log(rest)
        _mixer_block(N_META, N_META, layer == 0, hm_ref, rope_m_ref, dmat_m_ref, rtab_m_ref,
                     cdec_m_ref, om_ref, *shared)

    @pl.when(step > 0)
    def _():
        rows = hx_ref.shape[0]
        _mixer_block(rows, min(CHUNK, rows), layer == 0, hx_ref, rope_x_ref, dmat_x_ref,
                     rtab_x_ref, cdec_x_ref, ox_ref, *shared)


def _const_spec(shape):
    nd = len(shape)
    return pl.BlockSpec(shape, lambda b, t, _n=nd: (0,) * _n, pipeline_mode=pl.Buffered(1))


def _layer_spec(shape):
    nd = len(shape)

    def make(layer):
        return pl.BlockSpec((None,) + tuple(shape), lambda b, t, _l=layer, _n=nd: (_l,) + (0,) * _n,
                            pipeline_mode=pl.Buffered(1))
    return make


def _row_specs(tb):
    meta = pl.BlockSpec((None, N_META, D_MODEL), lambda b, t: (b, 0, 0))
    main = pl.BlockSpec((None, tb, D_MODEL), lambda b, t: (b, jnp.maximum(t - 1, 0), 0))
    return meta, main


def _mixer_call(layer, batch, seq, tb, chunk):
    nt = seq // tb
    meta_spec, main_spec = _row_specs(tb)
    in_specs = [
        meta_spec, main_spec,
        _const_spec((N_META, 3 * RET_QK)),
        pl.BlockSpec((tb, 3 * RET_QK), lambda b, t: (jnp.maximum(t - 1, 0), 0)),
        _const_spec((RET_HEADS, N_META, N_META)), _const_spec((RET_HEADS, chunk, chunk)),
        _const_spec((N_META, 2 * RET_QK)), _const_spec((chunk, 2 * RET_QK)),
        _const_spec((1, RET_V)), _const_spec((1, RET_V)),
        _layer_spec((1, D_MODEL))(layer),
        _layer_spec((D_MODEL, IN_COLS))(layer),
        _const_spec((4, HG_WIDTH)),
        _layer_spec((1, HG_WIDTH))(layer),
        _layer_spec((1, RET_V))(layer),
        _layer_spec((LRU_CONV, LRU_W))(layer),
        _layer_spec((1, LRU_W))(layer),
        _layer_spec((LRU_W, 2 * LRU_W))(layer),
        _layer_spec((1, 2 * LRU_W))(layer),
        _layer_spec((1, LRU_W))(layer),
        _layer_spec((D_MODEL, D_MODEL))(layer),
    ]
    return pl.pallas_call(
        functools.partial(_mixer_kernel, layer),
        grid=(batch, nt + 1),
        in_specs=in_specs,
        out_specs=list(_row_specs(tb)),
        out_shape=[jax.ShapeDtypeStruct((batch, N_META, D_MODEL), F32),
                   jax.ShapeDtypeStruct((batch, seq, D_MODEL), F32)],
        scratch_shapes=[
            pltpu.VMEM((HG_HEADS, HG_DIM, HG_DIM), F32),
            pltpu.VMEM((RET_QK, RET_V), F32),
            pltpu.VMEM((SUBLANES, LRU_W), F32),
            pltpu.VMEM((CONV_PAD + tb, LRU_W), F32),
            pltpu.VMEM((SUBLANES, HG_WIDTH), F32),
            pltpu.VMEM((tb, HG_WIDTH), F32),
            pltpu.VMEM((tb, HG_WIDTH), F32),
        ],
        compiler_params=pltpu.CompilerParams(
            dimension_semantics=("arbitrary", "arbitrary"), vmem_limit_bytes=VMEM_LIMIT),
        name=f"mixer_l{layer}",
    )


def _ffn_block(rows, last, h_ref, o_ref, ng_ref, wup_ref, cw_ref, cb_ref, wdn_ref, fg_ref,
               tail_ref, act_ref):
    h = h_ref[...]
    hn = _rmsnorm(h, ng_ref[...]).astype(BF16)

    def conv(cols):
        u = _dot(hn, wup_ref[:, cols])
        ext = jnp.concatenate([tail_ref[:, cols], u], axis=0)
        tail_ref[:, cols] = u[rows - CONV_PAD:rows, :]
        out = cb_ref[:, cols]
        for tap in range(FFN_CONV):
            off = CONV_PAD - (FFN_CONV - 1) + tap
            out = out + cw_ref[tap:tap + 1, cols] * ext[off:off + rows, :]
        return out

    for j in range(N_SLABS):
        gate = conv(slice(j * FFN_COLS, (j + 1) * FFN_COLS))
        val = conv(slice(D_FF + j * FFN_COLS, D_FF + (j + 1) * FFN_COLS))
        act_ref[0:rows, j * FFN_COLS:(j + 1) * FFN_COLS] = (_silu(gate) * val).astype(BF16)
    out = h + _dot(act_ref[0:rows, :], wdn_ref[...])
    if last:
        out = _rmsnorm(out, fg_ref[...])
    o_ref[...] = out


def _ffn_kernel(last, hm_ref, hx_ref, ng_ref, wup_ref, cw_ref, cb_ref, wdn_ref, fg_ref,
                om_ref, ox_ref, tail_ref, act_ref):
    step = pl.program_id(1)
    shared = (ng_ref, wup_ref, cw_ref, cb_ref, wdn_ref, fg_ref, tail_ref, act_ref)

    @pl.when(step == 0)
    def _():
        tail_ref[...] = jnp.zeros_like(tail_ref)
        _ffn_block(N_META, last, hm_ref, om_ref, *shared)

    @pl.when(step > 0)
    def _():
        _ffn_block(hx_ref.shape[0], last, hx_ref, ox_ref, *shared)


def _ffn_call(layer, last, batch, seq, tb):
    nt = seq // tb
    meta_spec, main_spec = _row_specs(tb)
    in_specs = [
        meta_spec, main_spec,
        _layer_spec((1, D_MODEL))(layer),
        _layer_spec((D_MODEL, 2 * D_FF))(layer),
        _layer_spec((FFN_CONV, 2 * D_FF))(layer),
        _layer_spec((1, 2 * D_FF))(layer),
        _layer_spec((D_FF, D_MODEL))(layer),
        _const_spec((1, D_MODEL)),
    ]
    return pl.pallas_call(
        functools.partial(_ffn_kernel, last),
        grid=(batch, nt + 1),
        in_specs=in_specs,
        out_specs=list(_row_specs(tb)),
        out_shape=[jax.ShapeDtypeStruct((batch, N_META, D_MODEL), F32),
                   jax.ShapeDtypeStruct((batch, seq, D_MODEL), F32)],
        scratch_shapes=[
            pltpu.VMEM((CONV_PAD, 2 * D_FF), F32),
            pltpu.VMEM((tb, D_FF), BF16),
        ],
        compiler_params=pltpu.CompilerParams(
            dimension_semantics=("arbitrary", "arbitrary"), vmem_limit_bytes=VMEM_LIMIT),
        name=f"ffn_l{layer}",
    )


def _rope_table(pos):
    theta = ROPE_BASE ** (-jnp.linspace(0.0, 1.0, RET_DK // 2, dtype=F32))
    ang = pos[:, None] * theta[None, :]
    cos, sin = jnp.cos(ang), jnp.sin(ang)
    zero = jnp.zeros_like(sin)
    cos_t = jnp.tile(jnp.concatenate([cos, cos], axis=-1), (1, RET_HEADS))
    sin_a = jnp.tile(jnp.concatenate([-sin, zero], axis=-1), (1, RET_HEADS))
    sin_b = jnp.tile(jnp.concatenate([zero, sin], axis=-1), (1, RET_HEADS))
    return jnp.concatenate([cos_t, sin_a, sin_b], axis=-1)


def _retention_tables(c):
    log_gamma = jnp.log1p(-jnp.exp2(-5.0 - jnp.arange(RET_HEADS, dtype=F32)))
    idx = jnp.arange(c, dtype=F32)
    lg = log_gamma[:, None]
    rel = idx[:, None] - idx[None, :]
    dmat = jnp.where(rel >= 0, jnp.exp(lg[:, :, None] * jnp.maximum(rel, 0.0)), 0.0)
    w_q = jnp.exp(lg * (idx + 1.0))
    w_s = jnp.exp(lg * (c - 1 - idx))
    rtab = jnp.concatenate([jnp.repeat(w_q.T, RET_DK, axis=1), jnp.repeat(w_s.T, RET_DK, axis=1)], axis=-1)
    cdec = jnp.repeat(jnp.exp(log_gamma * c), RET_DV)[None, :]
    return dmat, rtab, cdec


def _block_diag(w):
    depth, n, d, _ = w.shape
    eye = jnp.eye(n, dtype=w.dtype)
    return jnp.einsum('lnde,nm->lndme', w, eye).reshape(depth, n * d, n * d)


def kernel(x, meta_tokens, norm_mix_g, w_in, hg_lb_logits, hg_norm_g, ret_norm_g, lru_conv_w, lru_conv_b, lru_gate_a_w, lru_gate_a_b, lru_gate_i_w, lru_gate_i_b, lru_lambda, w_out, norm_ffn_g, ffn_w_up, ffn_conv_w, ffn_conv_b, ffn_w_down, final_norm_g):
    batch, seq, d_model = x.shape
    depth = w_in.shape[0]
    assert d_model == D_MODEL and w_in.shape[2] == IN_COLS and hg_lb_logits.shape == (4, HG_WIDTH)
    tb = min(SEQ_BLOCK, seq)
    chunk = min(CHUNK, tb)
    assert seq % tb == 0 and tb % chunk == 0 and chunk % DIAG == 0

    rope_m = _rope_table(jnp.arange(N_META, dtype=F32))
    rope_x = _rope_table(jnp.arange(N_META, N_META + seq, dtype=F32))
    dmat_m, rtab_m, cdec_m = _retention_tables(N_META)
    dmat_x, rtab_x, cdec_x = _retention_tables(chunk)

    w_in_b = w_in.astype(BF16)
    w_out_b = w_out.astype(BF16)
    w_gate = jnp.concatenate([_block_diag(lru_gate_a_w), _block_diag(lru_gate_i_w)], axis=-1).astype(BF16)
    gate_b = jnp.concatenate([lru_gate_a_b, lru_gate_i_b], axis=-1)[:, None, :]
    w_up_b = ffn_w_up.astype(BF16)
    conv_w = ffn_conv_w
    conv_b = ffn_conv_b[:, None, :]
    w_dn_b = ffn_w_down.astype(BF16)
    hgn = jnp.tile(hg_norm_g, (1, HG_HEADS))[:, None, :]
    rtn = jnp.tile(ret_norm_g, (1, RET_HEADS))[:, None, :]

    hm = jnp.broadcast_to(meta_tokens[None].astype(x.dtype), (batch, N_META, d_model))
    hx = x
    for layer in range(depth):
        hm, hx = _mixer_call(layer, batch, seq, tb, chunk)(
            hm, hx, rope_m, rope_x, dmat_m, dmat_x, rtab_m, rtab_x, cdec_m, cdec_x,
            norm_mix_g[:, None, :], w_in_b, hg_lb_logits, hgn, rtn, lru_conv_w,
            lru_conv_b[:, None, :], w_gate, gate_b, lru_lambda[:, None, :], w_out_b)
        hm, hx = _ffn_call(layer, layer == depth - 1, batch, seq, tb)(
            hm, hx, norm_ffn_g[:, None, :], w_up_b, conv_w, conv_b, w_dn_b, final_norm_g[None, :])
    return hx
```
